```python
import math
import jax, jax.numpy as jnp
from jax import lax
import numpy as np

D_MODEL = 4096
BATCH = 4
SEQ = 2048
DEPTH = 1

GRID_W = 64
HEAD_DIM = 128
Q_BLOCK = 128
EPS = 1e-6
A_HEADS = D_MODEL // (2 * HEAD_DIM)
A_QK = 2 * A_HEADS * HEAD_DIM
A_V = A_HEADS * 2 * HEAD_DIM
ROPE_THETA = 500000.0
ROT_DIM = HEAD_DIM // 4
B_Q_HEADS = D_MODEL // HEAD_DIM
B_KV_HEADS = B_Q_HEADS // 4
B_Q = B_Q_HEADS * HEAD_DIM
B_KV = B_KV_HEADS * HEAD_DIM
AXIAL_THETA = 10000.0
AXIS_DIM = HEAD_DIM // 2
SPLIT_SIZES = (A_QK, A_QK, A_V, B_Q, B_KV, B_KV, 2 * D_MODEL)
SPLIT_POINTS = tuple(int(v) for v in np.cumsum(SPLIT_SIZES)[:-1])
MIX_COLS = int(sum(SPLIT_SIZES))
N_EXPERTS = 32
TOP_K = 4
D_EXPERT = D_MODEL // 4
SWIGLU_LIMIT = 7.0
SWIGLU_ALPHA = 1.702

kernel_name = "hybrid_gated_diffattn_axialgqa_moe"


def rmsnorm(x, g):
    x32 = x.astype(jnp.float32)
    y = x32 * lax.rsqrt(jnp.mean(x32 * x32, axis=-1, keepdims=True) + EPS)
    return (y * g.astype(jnp.float32)).astype(x.dtype)


def rope(x, pos, theta):
    d = x.shape[-1]
    half = d // 2
    inv = jnp.power(jnp.float32(theta), -2.0 * jnp.arange(half, dtype=jnp.float32) / d)
    ang = pos.astype(jnp.float32)[:, None] * inv[None, :]
    cos = jnp.cos(ang)[None, :, None, :]
    sin = jnp.sin(ang)[None, :, None, :]
    x32 = x.astype(jnp.float32)
    x1, x2 = x32[..., :half], x32[..., half:]
    return jnp.concatenate([x1 * cos - x2 * sin, x2 * cos + x1 * sin], axis=-1).astype(x.dtype)


def to_blocks(q):
    b, s = q.shape[:2]
    return q.reshape((b, s // Q_BLOCK, Q_BLOCK) + q.shape[2:]).swapaxes(0, 1)


def from_blocks(o):
    nb, b, qb = o.shape[:3]
    return o.swapaxes(0, 1).reshape((b, nb * qb) + o.shape[3:])


def diff_attention(q, k, v, lam):
    scale = HEAD_DIM ** -0.5
    b = q.shape[0]

    def block(qb):
        s = jnp.einsum('bqhd,bkhd->bhqk', qb, k, preferred_element_type=jnp.float32) * scale
        p = jax.nn.softmax(s, axis=-1)
        p = p.reshape((b, A_HEADS, 2) + p.shape[2:])
        a = p[:, :, 0] - lam * p[:, :, 1]
        return jnp.einsum('bhqk,bkhe->bqhe', a.astype(v.dtype), v)

    return from_blocks(lax.map(block, to_blocks(q)))


def gqa_attention(q, k, v):
    scale = HEAD_DIM ** -0.5
    b, s = q.shape[:2]
    q = q.reshape(b, s, B_KV_HEADS, B_Q_HEADS // B_KV_HEADS, HEAD_DIM)

    def block(qb):
        sc = jnp.einsum('bqgrd,bkgd->bgrqk', qb, k, preferred_element_type=jnp.float32) * scale
        p = jax.nn.softmax(sc, axis=-1)
        return jnp.einsum('bgrqk,bkgd->bqgrd', p.astype(v.dtype), v)

    o = from_blocks(lax.map(block, to_blocks(q)))
    return o.reshape(b, s, B_Q)


def token_mixers(h, w_in, b_bg, subln, qn, kn, lam_p, wpa, wpb, wo, layer_idx):
    b, s, _ = h.shape
    proj = h @ w_in
    qa, ka, va, qb, kb, vb, gate_logits = jnp.split(proj, SPLIT_POINTS, axis=-1)

    rows = s // GRID_W
    pos = jnp.arange(s, dtype=jnp.int32)
    row_idx = jnp.repeat(jnp.arange(rows, dtype=jnp.int32), GRID_W)
    col_idx = jnp.tile(jnp.arange(GRID_W, dtype=jnp.int32), rows)

    qa = qa.reshape(b, s, 2 * A_HEADS, HEAD_DIM)
    ka = ka.reshape(b, s, 2 * A_HEADS, HEAD_DIM)
    va = va.reshape(b, s, A_HEADS, 2 * HEAD_DIM)
    qa = jnp.concatenate([rope(qa[..., :ROT_DIM], pos, ROPE_THETA), qa[..., ROT_DIM:]], axis=-1)
    ka = jnp.concatenate([rope(ka[..., :ROT_DIM], pos, ROPE_THETA), ka[..., ROT_DIM:]], axis=-1)
    lambda_init = 0.8 - 0.6 * math.exp(-0.3 * layer_idx)
    lp = lam_p.astype(jnp.float32)
    lam = jnp.exp(jnp.sum(lp[0] * lp[1])) - jnp.exp(jnp.sum(lp[2] * lp[3])) + lambda_init
    oa = diff_attention(qa, ka, va, lam)
    oa = (rmsnorm(oa, subln) * (1.0 - lambda_init)).reshape(b, s, A_V)

    qb = rmsnorm(qb.reshape(b, s, B_Q_HEADS, HEAD_DIM), qn)
    kb = rmsnorm(kb.reshape(b, s, B_KV_HEADS, HEAD_DIM), kn)
    vb = vb.reshape(b, s, B_KV_HEADS, HEAD_DIM)
    qb = jnp.concatenate([rope(qb[..., :AXIS_DIM], row_idx, AXIAL_THETA),
                          rope(qb[..., AXIS_DIM:], col_idx, AXIAL_THETA)], axis=-1)
    kb = jnp.concatenate([rope(kb[..., :AXIS_DIM], row_idx, AXIAL_THETA),
                          rope(kb[..., AXIS_DIM:], col_idx, AXIAL_THETA)], axis=-1)
    ob = gqa_attention(qb, kb, vb)

    gates = jax.nn.sigmoid((gate_logits + b_bg).astype(jnp.float32)).astype(h.dtype)
    g_a, g_b = gates[..., :D_MODEL], gates[..., D_MODEL:]
    merged = g_a * (oa @ wpa) + g_b * (ob @ wpb)
    return merged @ wo


def moe_ffn(h, w_router, b_router, w_gate, b_gate, w_up, b_up, w_down, b_down):
    b, s, d = h.shape
    hf = h.reshape(b * s, d)
    logits = (hf @ w_router).astype(jnp.float32) + b_router.astype(jnp.float32)
    top_vals, top_idx = lax.top_k(logits, TOP_K)
    probs = jax.nn.softmax(top_vals, axis=-1)
    combine = jnp.sum(jax.nn.one_hot(top_idx, N_EXPERTS, dtype=jnp.float32) * probs[..., None], axis=1)
    combine = combine.astype(h.dtype)
    y = jnp.zeros_like(hf)
    for e in range(N_EXPERTS):
        g = jnp.minimum(hf @ w_gate[e] + b_gate[e], SWIGLU_LIMIT)
        u = jnp.clip(hf @ w_up[e] + b_up[e], -SWIGLU_LIMIT, SWIGLU_LIMIT)
        act = g * jax.nn.sigmoid(SWIGLU_ALPHA * g) * (u + 1.0)
        y = y + combine[:, e:e + 1] * (act @ w_down[e] + b_down[e])
    return y.reshape(b, s, d)


def setup_inputs(seed: int = 0) -> dict:
    key = jax.random.key(seed)
    ks = jax.random.split(key, 24)
    f32 = jnp.float32
    L, D, E, F = DEPTH, D_MODEL, N_EXPERTS, D_EXPERT

    def nrm(k, shape, scale):
        return jax.random.normal(k, shape, f32) * scale

    return {
        "x": nrm(ks[0], (BATCH, SEQ, D), 1.0),
        "w_in": nrm(ks[1], (L, D, MIX_COLS), D ** -0.5),
        "b_branch_gate": nrm(ks[2], (L, 2 * D), 0.01),
        "norm_mix": 1.0 + nrm(ks[3], (L, D), 0.02),
        "subln_a": 1.0 + nrm(ks[4], (L, 2 * HEAD_DIM), 0.02),
        "qnorm_b": 1.0 + nrm(ks[5], (L, HEAD_DIM), 0.02),
        "knorm_b": 1.0 + nrm(ks[6], (L, HEAD_DIM), 0.02),
        "diff_lambda": nrm(ks[7], (L, 4, HEAD_DIM), 0.1),
        "w_proj_a": nrm(ks[8], (L, A_V, D), A_V ** -0.5),
        "w_proj_b": nrm(ks[9], (L, B_Q, D), B_Q ** -0.5),
        "w_out": nrm(ks[10], (L, D, D), D ** -0.5),
        "norm_ffn": 1.0 + nrm(ks[11], (L, D), 0.02),
        "w_router": nrm(ks[12], (L, D, E), D ** -0.5),
        "b_router": nrm(ks[13], (L, E), 0.01),
        "w_gate": nrm(ks[14], (L, E, D, F), D ** -0.5),
        "b_gate": nrm(ks[15], (L, E, F), 0.01),
        "w_up": nrm(ks[16], (L, E, D, F), D ** -0.5),
        "b_up": nrm(ks[17], (L, E, F), 0.01),
        "w_down": nrm(ks[18], (L, E, F, D), F ** -0.5),
        "b_down": nrm(ks[19], (L, E, D), 0.01),
        "norm_final": 1.0 + nrm(ks[20], (D,), 0.02),
    }


def reference(x, w_in, b_branch_gate, norm_mix, subln_a, qnorm_b, knorm_b, diff_lambda,
              w_proj_a, w_proj_b, w_out, norm_ffn, w_router, b_router, w_gate, b_gate,
              w_up, b_up, w_down, b_down, norm_final):
    for l in range(DEPTH):
        h = rmsnorm(x, norm_mix[l])
        x = x + token_mixers(h, w_in[l], b_branch_gate[l], subln_a[l], qnorm_b[l], knorm_b[l],
                             diff_lambda[l], w_proj_a[l], w_proj_b[l], w_out[l], l)
        h = rmsnorm(x, norm_ffn[l])
        x = x + moe_ffn(h, w_router[l], b_router[l], w_gate[l], b_gate[l], w_up[l], b_up[l],
                        w_down[l], b_down[l])
    return rmsnorm(x, norm_final)
```

```python
import collections
import functools
import math

import jax
import jax.numpy as jnp
from jax import lax
from jax.experimental import pallas as pl
from jax.experimental.pallas import tpu as pltpu

F32 = jnp.float32
BF16 = jnp.bfloat16
U32 = jnp.uint32

EPS = 1e-6
HEAD_DIM = 128
ROPE_THETA = 500000.0
ROT_DIM = HEAD_DIM // 4
AXIAL_THETA = 10000.0
AXIS_DIM = HEAD_DIM // 2
TOP_K = 4
SWIGLU_LIMIT = 7.0
SWIGLU_ALPHA = 1.702
LAMBDA_INIT = 0.8 - 0.6 * math.exp(-0.3 * 0)
LOG2E = 1.4426950408889634
Q_SCALE = HEAD_DIM ** -0.5 * LOG2E

V7X_VMEM_LIMIT_BYTES = 56 * 1024 * 1024

Cfg = collections.namedtuple(
    "Cfg",
    "batch seq d_model grid_w n_experts d_expert "
    "tm_proj tn_proj tm_merge tn_merge tq rows_norm tm_moe fc_moe rows_gather")

FULL_CFG = Cfg(batch=4, seq=2048, d_model=4096, grid_w=64, n_experts=32, d_expert=1024,
               tm_proj=1024, tn_proj=512, tm_merge=256, tn_merge=512, tq=256, rows_norm=256,
               tm_moe=256, fc_moe=512, rows_gather=256)


def _params(*sem):
    return pltpu.CompilerParams(dimension_semantics=sem, vmem_limit_bytes=V7X_VMEM_LIMIT_BYTES)


def _rmsnorm_kernel(x_ref, g_ref, o_ref):
    x = x_ref[...]
    ms = jnp.mean(x * x, axis=-1, keepdims=True)
    o_ref[...] = (x * lax.rsqrt(ms + EPS) * g_ref[...]).astype(o_ref.dtype)


def _rmsnorm(x, g, rows, out_dtype):
    t, d = x.shape
    return pl.pallas_call(
        _rmsnorm_kernel,
        grid=(t // rows,),
        in_specs=[pl.BlockSpec((rows, d), lambda i: (i, 0)),
                  pl.BlockSpec((1, d), lambda i: (0, 0))],
        out_specs=pl.BlockSpec((rows, d), lambda i: (i, 0)),
        out_shape=jax.ShapeDtypeStruct((t, d), out_dtype),
        compiler_params=_params("arbitrary"),
        name="rmsnorm",
    )(x, g.reshape(1, d))


def _cast_rows(src_ref, dst_ref, rows_per):
    n = src_ref.shape[0] // rows_per

    def body(i, carry):
        r = pl.multiple_of(i * rows_per, rows_per)
        dst_ref[pl.ds(r, rows_per), :] = src_ref[pl.ds(r, rows_per), :].astype(BF16)
        return carry

    lax.fori_loop(0, n, body, 0)


def _rope_mix(x, c, s1, s2, shift):
    return (x * c + pltpu.roll(x, HEAD_DIM - shift, 1) * s1 + pltpu.roll(x, shift, 1) * s2)


def _mm_kernel(*refs, n_pairs, mode, tn):
    lhs = refs[:n_pairs]
    w = refs[n_pairs:2 * n_pairs]
    n_scratch = n_pairs
    extra = refs[2 * n_pairs:len(refs) - 1 - n_scratch]
    o_ref = refs[len(refs) - 1 - n_scratch]
    wb = refs[len(refs) - n_scratch:]

    @pl.when(pl.program_id(1) == 0)
    def _():
        for p in range(n_pairs):
            _cast_rows(w[p], wb[p], 512 if w[p].shape[0] % 512 == 0 else w[p].shape[0])

    accs = [jnp.dot(lhs[p][...], wb[p][...], preferred_element_type=F32) for p in range(n_pairs)]

    if mode == "plain":
        o_ref[...] = accs[0].astype(o_ref.dtype)
    elif mode == "rope":
        c_ref, s1_ref, s2_ref = extra
        c, s1, s2 = c_ref[...], s1_ref[...], s2_ref[...]
        for hd in range(tn // HEAD_DIM):
            sl = slice(hd * HEAD_DIM, (hd + 1) * HEAD_DIM)
            o_ref[:, sl] = _rope_mix(accs[0][:, sl], c, s1, s2, ROT_DIM // 2).astype(o_ref.dtype)
    elif mode == "normrope":
        g_ref, c_ref, s1_ref, s2_ref = extra
        c, s1, s2 = c_ref[...], s1_ref[...], s2_ref[...]
        for hd in range(tn // HEAD_DIM):
            sl = slice(hd * HEAD_DIM, (hd + 1) * HEAD_DIM)
            x = accs[0][:, sl]
            ms = jnp.mean(x * x, axis=-1, keepdims=True)
            y = x * lax.rsqrt(ms + EPS) * g_ref[:, sl]
            o_ref[:, sl] = _rope_mix(y, c, s1, s2, AXIS_DIM // 2).astype(o_ref.dtype)
    elif mode == "gate":
        (b_ref,) = extra
        z = accs[0] + b_ref[...]
        o_ref[...] = (1.0 / (1.0 + jnp.exp(-z))).astype(o_ref.dtype)
    elif mode == "merge":
        ga_ref, gb_ref = extra
        o_ref[...] = (ga_ref[...].astype(F32) * accs[0]
                      + gb_ref[...].astype(F32) * accs[1]).astype(o_ref.dtype)
    elif mode == "residual":
        (x_ref,) = extra
        o_ref[...] = (x_ref[...] + accs[0]).astype(o_ref.dtype)
    else:
        raise ValueError(mode)


def _mm(lhs_list, w_list, col_off, n_cols, extra, extra_specs, mode, tm, tn, out_dtype, name):
    t, k = lhs_list[0].shape
    n_pairs = len(lhs_list)
    assert col_off % tn == 0 and n_cols % tn == 0 and t % tm == 0
    off = col_off // tn
    in_specs = ([pl.BlockSpec((tm, k), lambda j, i: (i, 0)) for _ in lhs_list]
                + [pl.BlockSpec((k, tn), lambda j, i: (0, off + j)) for _ in w_list]
                + list(extra_specs))
    return pl.pallas_call(
        functools.partial(_mm_kernel, n_pairs=n_pairs, mode=mode, tn=tn),
        grid=(n_cols // tn, t // tm),
        in_specs=in_specs,
        out_specs=pl.BlockSpec((tm, tn), lambda j, i: (i, j)),
        out_shape=jax.ShapeDtypeStruct((t, n_cols), out_dtype),
        scratch_shapes=[pltpu.VMEM((k, tn), BF16) for _ in w_list],
        compiler_params=_params("arbitrary", "arbitrary"),
        name=name,
    )(*lhs_list, *w_list, *extra)


def _softmax_pv(q, k, v):
    s = lax.dot_general(q, k, (((1,), (1,)), ((), ())), preferred_element_type=F32)
    m = jnp.max(s, axis=-1, keepdims=True)
    e = jnp.exp2(s - m)
    l = jnp.sum(e, axis=-1, keepdims=True)
    return jnp.dot(e.astype(BF16), v, preferred_element_type=F32), l


def _attn_a_kernel(lam_ref, q_ref, k_ref, v_ref, g_ref, o_ref):
    lp = lam_ref[...]
    lam = (jnp.exp(jnp.sum(lp[0:1] * lp[1:2], axis=-1, keepdims=True))
           - jnp.exp(jnp.sum(lp[2:3] * lp[3:4], axis=-1, keepdims=True)) + LAMBDA_INIT)
    v = v_ref[...]
    comps = []
    for c in range(2):
        sl = slice(c * HEAD_DIM, (c + 1) * HEAD_DIM)
        pv, l = _softmax_pv(q_ref[:, sl], k_ref[:, sl], v)
        comps.append(pv * (1.0 / l))
    o = comps[0] - lam * comps[1]
    ms = jnp.mean(o * o, axis=-1, keepdims=True)
    o_ref[...] = (o * lax.rsqrt(ms + EPS) * g_ref[...]).astype(o_ref.dtype)


def _attn_a(qk, va, diff_lambda, gain, cfg):
    heads = cfg.d_model // (2 * HEAD_DIM)
    hw = 2 * HEAD_DIM
    nq = cfg.seq // cfg.tq
    return pl.pallas_call(
        _attn_a_kernel,
        grid=(cfg.batch, heads, nq),
        in_specs=[pl.BlockSpec((4, HEAD_DIM), lambda b, h, i: (0, 0)),
                  pl.BlockSpec((cfg.tq, hw), lambda b, h, i: (b * nq + i, h)),
                  pl.BlockSpec((cfg.seq, hw), lambda b, h, i: (b, heads + h)),
                  pl.BlockSpec((cfg.seq, hw), lambda b, h, i: (b, h)),
                  pl.BlockSpec((1, hw), lambda b, h, i: (0, 0))],
        out_specs=pl.BlockSpec((cfg.tq, hw), lambda b, h, i: (b * nq + i, h)),
        out_shape=jax.ShapeDtypeStruct((cfg.batch * cfg.seq, cfg.d_model), BF16),
        compiler_params=_params("arbitrary", "arbitrary", "arbitrary"),
        name="diff_attention",
    )(diff_lambda, qk, qk, va, gain)


def _attn_b_kernel(q_ref, k_ref, v_ref, o_ref):
    k = k_ref[...]
    v = v_ref[...]
    for r in range(4):
        sl = slice(r * HEAD_DIM, (r + 1) * HEAD_DIM)
        pv, l = _softmax_pv(q_ref[:, sl], k, v)
        o_ref[:, sl] = (pv * (1.0 / l)).astype(o_ref.dtype)


def _attn_b(qk, vb, cfg):
    kv_heads = cfg.d_model // HEAD_DIM // 4
    gw = 4 * HEAD_DIM
    nq = cfg.seq // cfg.tq
    return pl.pallas_call(
        _attn_b_kernel,
        grid=(cfg.batch, kv_heads, nq),
        in_specs=[pl.BlockSpec((cfg.tq, gw), lambda b, g, i: (b * nq + i, g)),
                  pl.BlockSpec((cfg.seq, HEAD_DIM), lambda b, g, i: (b, 4 * kv_heads + g)),
                  pl.BlockSpec((cfg.seq, HEAD_DIM), lambda b, g, i: (b, g))],
        out_specs=pl.BlockSpec((cfg.tq, gw), lambda b, g, i: (b * nq + i, g)),
        out_shape=jax.ShapeDtypeStruct((cfg.batch * cfg.seq, cfg.d_model), BF16),
        compiler_params=_params("arbitrary", "arbitrary", "arbitrary"),
        name="gqa_attention",
    )(qk, qk, vb)


def _pack_bf16_pair(lo, hi):
    lo_bits = lax.bitcast_convert_type(lo.astype(BF16).astype(F32), U32)
    hi_bits = lax.bitcast_convert_type(hi.astype(BF16).astype(F32), U32)
    return (hi_bits & jnp.uint32(0xFFFF0000)) | (lo_bits >> 16)


def _unpack_bf16_pair(w):
    lo = lax.bitcast_convert_type(w << 16, F32)
    hi = lax.bitcast_convert_type(w & jnp.uint32(0xFFFF0000), F32)
    return lo, hi


def _router_kernel(x_ref, g_ref, wr_ref, br_ref, hp_ref, sel_ref, comb_ref, rank_ref, cnt_ref):
    rows, d = x_ref.shape
    n_exp = wr_ref.shape[1]

    @pl.when(pl.program_id(0) == 0)
    def _():
        cnt_ref[...] = jnp.zeros_like(cnt_ref)

    x = x_ref[...]
    ms = jnp.mean(x * x, axis=-1, keepdims=True)
    h = x * lax.rsqrt(ms + EPS) * g_ref[...]
    hp_ref[...] = _pack_bf16_pair(h[:, :d // 2], h[:, d // 2:])

    logits = jnp.dot(h, wr_ref[...], preferred_element_type=F32,
                     precision=lax.Precision.HIGHEST) + br_ref[...]
    lane = lax.broadcasted_iota(jnp.int32, (rows, n_exp), 1).astype(F32)
    work = logits
    one_hots, vals = [], []
    for _ in range(TOP_K):
        m = jnp.max(work, axis=-1, keepdims=True)
        idx = jnp.min(jnp.where(work == m, lane, float(n_exp)), axis=-1, keepdims=True)
        oh = lane == idx
        one_hots.append(oh)
        vals.append(m)
        work = jnp.where(oh, -jnp.inf, work)
    exps = [jnp.exp(v - vals[0]) for v in vals]
    inv = 1.0 / (exps[0] + exps[1] + exps[2] + exps[3])
    comb = jnp.zeros((rows, n_exp), F32)
    sel = jnp.zeros((rows, n_exp), F32)
    for oh, e in zip(one_hots, exps):
        comb = jnp.where(oh, e * inv, comb)
        sel = jnp.where(oh, 1.0, sel)
    comb_ref[...] = comb
    sel_ref[...] = sel

    r_i = lax.broadcasted_iota(jnp.int32, (rows, rows), 0)
    c_i = lax.broadcasted_iota(jnp.int32, (rows, rows), 1)
    lower = jnp.where(c_i < r_i, 1.0, 0.0).astype(BF16)
    within = jnp.dot(lower, sel.astype(BF16), preferred_element_type=F32)
    rank_ref[...] = within + cnt_ref[...]
    cnt_ref[...] = cnt_ref[...] + jnp.sum(sel, axis=0, keepdims=True)


def _router(x1, g, w_router, b_router, cfg):
    t, d = x1.shape
    e = cfg.n_experts
    rows = cfg.rows_norm
    tile = lambda i: (i, 0)
    fixed = lambda i: (0, 0)
    return pl.pallas_call(
        _router_kernel,
        grid=(t // rows,),
        in_specs=[pl.BlockSpec((rows, d), tile), pl.BlockSpec((1, d), fixed),
                  pl.BlockSpec((d, e), fixed), pl.BlockSpec((1, e), fixed)],
        out_specs=[pl.BlockSpec((rows, d // 2), tile), pl.BlockSpec((rows, e), tile),
                   pl.BlockSpec((rows, e), tile), pl.BlockSpec((rows, e), tile),
                   pl.BlockSpec((1, e), fixed)],
        out_shape=[jax.ShapeDtypeStruct((t, d // 2), U32), jax.ShapeDtypeStruct((t, e), F32),
                   jax.ShapeDtypeStruct((t, e), F32), jax.ShapeDtypeStruct((t, e), F32),
                   jax.ShapeDtypeStruct((1, e), F32)],
        compiler_params=_params("arbitrary"),
        name="ffn_norm_router",
    )(x1, g.reshape(1, d), w_router, b_router.reshape(1, e))


def _row_copy(src_hbm, row, dst_ref, slot, sem):
    return pltpu.make_async_copy(src_hbm.at[pl.ds(row, 1), :], dst_ref.at[pl.ds(slot, 1), :], sem)


def _dispatch_kernel(src_ref, hp_hbm, o_ref, sem):
    rows = o_ref.shape[0]
    base = pl.program_id(0) * rows

    def issue(r, carry):
        _row_copy(hp_hbm, src_ref[base + r], o_ref, r, sem).start()
        return carry

    lax.fori_loop(0, rows, issue, 0)
    pltpu.make_async_copy(hp_hbm.at[pl.ds(0, rows), :], o_ref, sem).wait()


def _dispatch(src, hp, n_rows, cfg):
    rows = cfg.rows_gather
    w = hp.shape[1]
    return pl.pallas_call(
        _dispatch_kernel,
        grid_spec=pltpu.PrefetchScalarGridSpec(
            num_scalar_prefetch=1,
            grid=(n_rows // rows,),
            in_specs=[pl.BlockSpec(memory_space=pl.ANY)],
            out_specs=pl.BlockSpec((rows, w), lambda i, src: (i, 0)),
            scratch_shapes=[pltpu.SemaphoreType.DMA(())]),
        out_shape=jax.ShapeDtypeStruct((n_rows, w), U32),
        compiler_params=_params("arbitrary"),
        name="moe_dispatch",
    )(src, hp)


def _is_first_tile_of_expert(te_ref, t):
    prev = te_ref[jnp.maximum(t - 1, 0)]
    return jnp.logical_or(t == 0, prev != te_ref[t])


def _expert_up_kernel(te_ref, tv_ref, xs_ref, wg_ref, wu_ref, bg_ref, bu_ref, o_ref, wgb, wub):
    t = pl.program_id(1)
    half = xs_ref.shape[1]

    @pl.when(_is_first_tile_of_expert(te_ref, t))
    def _():
        _cast_rows(wg_ref, wgb, 512 if wg_ref.shape[0] % 512 == 0 else wg_ref.shape[0])
        _cast_rows(wu_ref, wub, 512 if wu_ref.shape[0] % 512 == 0 else wu_ref.shape[0])

    @pl.when(tv_ref[t] == 1)
    def _():
        lo, hi = _unpack_bf16_pair(xs_ref[...])
        lo = lo.astype(BF16)
        hi = hi.astype(BF16)

        def proj(wb, b_ref):
            return (jnp.dot(lo, wb[:half, :], preferred_element_type=F32)
                    + jnp.dot(hi, wb[half:, :], preferred_element_type=F32) + b_ref[...])

        g = jnp.minimum(proj(wgb, bg_ref), SWIGLU_LIMIT)
        u = jnp.clip(proj(wub, bu_ref), -SWIGLU_LIMIT, SWIGLU_LIMIT)
        act = g * (1.0 / (1.0 + jnp.exp(-SWIGLU_ALPHA * g))) * (u + 1.0)
        o_ref[...] = act.astype(o_ref.dtype)

    @pl.when(tv_ref[t] == 0)
    def _():
        o_ref[...] = jnp.zeros_like(o_ref)


def _expert_up(tile_expert, tile_valid, xs, w_gate, w_up, b_gate, b_up, cfg):
    n_rows, half = xs.shape
    e, d, f = w_gate.shape
    tm, fc = cfg.tm_moe, cfg.fc_moe
    w_spec = pl.BlockSpec((None, d, fc), lambda c, t, te, tv: (te[t], 0, c))
    b_spec = pl.BlockSpec((None, 1, fc), lambda c, t, te, tv: (te[t], 0, c))
    return pl.pallas_call(
        _expert_up_kernel,
        grid_spec=pltpu.PrefetchScalarGridSpec(
            num_scalar_prefetch=2,
            grid=(f // fc, n_rows // tm),
            in_specs=[pl.BlockSpec((tm, half), lambda c, t, te, tv: (t, 0)),
                      w_spec, w_spec, b_spec, b_spec],
            out_specs=pl.BlockSpec((tm, fc), lambda c, t, te, tv: (t, c)),
            scratch_shapes=[pltpu.VMEM((d, fc), BF16), pltpu.VMEM((d, fc), BF16)]),
        out_shape=jax.ShapeDtypeStruct((n_rows, f), BF16),
        compiler_params=_params("arbitrary", "arbitrary"),
        name="moe_expert_up",
    )(tile_expert, tile_valid, xs, w_gate, w_up, b_gate.reshape(e, 1, f), b_up.reshape(e, 1, f))


def _expert_down_kernel(te_ref, tv_ref, a_ref, wd_ref, bd_ref, o_ref, wdb):
    t = pl.program_id(0)
    half = o_ref.shape[1]

    @pl.when(_is_first_tile_of_expert(te_ref, t))
    def _():
        _cast_rows(wd_ref, wdb, 512 if wd_ref.shape[0] % 512 == 0 else wd_ref.shape[0])

    @pl.when(tv_ref[t] == 1)
    def _():
        y = jnp.dot(a_ref[...], wdb[...], preferred_element_type=F32) + bd_ref[...]
        o_ref[...] = _pack_bf16_pair(y[:, :half], y[:, half:])

    @pl.when(tv_ref[t] == 0)
    def _():
        o_ref[...] = jnp.zeros_like(o_ref)


def _expert_down(tile_expert, tile_valid, act, w_down, b_down, cfg):
    n_rows, f = act.shape
    e, _, d = w_down.shape
    tm = cfg.tm_moe
    return pl.pallas_call(
        _expert_down_kernel,
        grid_spec=pltpu.PrefetchScalarGridSpec(
            num_scalar_prefetch=2,
            grid=(n_rows // tm,),
            in_specs=[pl.BlockSpec((tm, f), lambda t, te, tv: (t, 0)),
                      pl.BlockSpec((None, f, d), lambda t, te, tv: (te[t], 0, 0)),
                      pl.BlockSpec((None, 1, d), lambda t, te, tv: (te[t], 0, 0))],
            out_specs=pl.BlockSpec((tm, d // 2), lambda t, te, tv: (t, 0)),
            scratch_shapes=[pltpu.VMEM((f, d), BF16)]),
        out_shape=jax.ShapeDtypeStruct((n_rows, d // 2), U32),
        compiler_params=_params("arbitrary"),
        name="moe_expert_down",
    )(tile_expert, tile_valid, act, w_down, b_down.reshape(e, 1, d))


def _combine_kernel(dest_ref, ys_hbm, p_ref, x_ref, g_ref, o_ref, buf, sem):
    rows = x_ref.shape[0]
    base = pl.program_id(0) * rows

    def issue(r, carry):
        for k in range(TOP_K):
            _row_copy(ys_hbm, dest_ref[(base + r) * TOP_K + k], buf.at[k], r, sem).start()
        return carry

    lax.fori_loop(0, rows, issue, 0)
    for k in range(TOP_K):
        pltpu.make_async_copy(ys_hbm.at[pl.ds(0, rows), :], buf.at[k], sem).wait()

    p = p_ref[...]
    y_lo = jnp.zeros(buf.shape[1:], F32)
    y_hi = jnp.zeros(buf.shape[1:], F32)
    for k in range(TOP_K):
        lo, hi = _unpack_bf16_pair(buf[k])
        y_lo = y_lo + p[:, k:k + 1] * lo
        y_hi = y_hi + p[:, k:k + 1] * hi
    x = x_ref[...] + jnp.concatenate([y_lo, y_hi], axis=-1)
    ms = jnp.mean(x * x, axis=-1, keepdims=True)
    o_ref[...] = x * lax.rsqrt(ms + EPS) * g_ref[...]


def _combine(dest, ys, probs, x1, g, cfg):
    t, d = x1.shape
    rows = cfg.rows_gather
    return pl.pallas_call(
        _combine_kernel,
        grid_spec=pltpu.PrefetchScalarGridSpec(
            num_scalar_prefetch=1,
            grid=(t // rows,),
            in_specs=[pl.BlockSpec(memory_space=pl.ANY),
                      pl.BlockSpec((rows, TOP_K), lambda i, dst: (i, 0)),
                      pl.BlockSpec((rows, d), lambda i, dst: (i, 0)),
                      pl.BlockSpec((1, d), lambda i, dst: (0, 0))],
            out_specs=pl.BlockSpec((rows, d), lambda i, dst: (i, 0)),
            scratch_shapes=[pltpu.VMEM((TOP_K, rows, d // 2), U32), pltpu.SemaphoreType.DMA(())]),
        out_shape=jax.ShapeDtypeStruct((t, d), F32),
        compiler_params=_params("arbitrary"),
        name="moe_combine_final_norm",
    )(dest, ys, probs, x1, g.reshape(1, d))


def _rotary_tables(pos_lists, thetas, half, scale):
    s = pos_lists[0].shape[0]
    d = 2 * half
    c = jnp.ones((s, HEAD_DIM), F32)
    s1 = jnp.zeros((s, HEAD_DIM), F32)
    s2 = jnp.zeros((s, HEAD_DIM), F32)
    for a, (pos, theta) in enumerate(zip(pos_lists, thetas)):
        inv = jnp.power(jnp.float32(theta), -2.0 * jnp.arange(half, dtype=F32) / d)
        ang = pos.astype(F32)[:, None] * inv[None, :]
        cos, sin = jnp.cos(ang), jnp.sin(ang)
        o = a * d
        c = c.at[:, o:o + half].set(cos).at[:, o + half:o + d].set(cos)
        s1 = s1.at[:, o:o + half].set(-sin)
        s2 = s2.at[:, o + half:o + d].set(sin)
    return c * scale, s1 * scale, s2 * scale


def _forward(cfg, x, w_in, b_branch_gate, norm_mix, subln_a, qnorm_b, knorm_b, diff_lambda,
             w_proj_a, w_proj_b, w_out, norm_ffn, w_router, b_router, w_gate, b_gate,
             w_up, b_up, w_down, b_down, norm_final):
    b, s, d = cfg.batch, cfg.seq, cfg.d_model
    t = b * s
    a_qk = d
    b_q = d
    b_kv = d // 4
    x2 = x.reshape(t, d)
    w_in0 = w_in[0]

    h = _rmsnorm(x2, norm_mix[0], cfg.rows_norm, BF16)

    tm, tn = cfg.tm_proj, cfg.tn_proj
    blocks_per_seq = s // tm
    pos = jnp.arange(s, dtype=jnp.int32)
    ta_q = _rotary_tables([pos], [ROPE_THETA], ROT_DIM // 2, Q_SCALE)
    ta_k = _rotary_tables([pos], [ROPE_THETA], ROT_DIM // 2, 1.0)
    tb_q = _rotary_tables([pos // cfg.grid_w, pos % cfg.grid_w], [AXIAL_THETA] * 2, AXIS_DIM // 2, Q_SCALE)
    tb_k = _rotary_tables([pos // cfg.grid_w, pos % cfg.grid_w], [AXIAL_THETA] * 2, AXIS_DIM // 2, 1.0)

    def table_specs(n_q_tiles):
        imap = lambda j, i: ((j >= n_q_tiles) * blocks_per_seq + i % blocks_per_seq, 0)
        return [pl.BlockSpec((tm, HEAD_DIM), imap)] * 3

    tabs_a = [jnp.concatenate([q_, k_], axis=0) for q_, k_ in zip(ta_q, ta_k)]
    tabs_b = [jnp.concatenate([q_, k_], axis=0) for q_, k_ in zip(tb_q, tb_k)]

    col = 0
    qk_a = _mm([h], [w_in0], col, 2 * a_qk, tabs_a, table_specs(a_qk // tn), "rope", tm, tn, BF16,
               "proj_qk_a")
    col += 2 * a_qk
    v_a = _mm([h], [w_in0], col, d, [], [], "plain", tm, tn, BF16, "proj_v_a")
    col += d
    gain_b = jnp.concatenate([jnp.tile(qnorm_b[0], b_q // HEAD_DIM),
                              jnp.tile(knorm_b[0], b_kv // HEAD_DIM)]).reshape(1, b_q + b_kv)
    qk_b = _mm([h], [w_in0], col, b_q + b_kv, [gain_b] + tabs_b,
               [pl.BlockSpec((1, tn), lambda j, i: (0, j))] + table_specs(b_q // tn),
               "normrope", tm, tn, BF16, "proj_qk_b")
    col += b_q + b_kv
    v_b = _mm([h], [w_in0], col, b_kv, [], [], "plain", tm, tn, BF16, "proj_v_b")
    col += b_kv
    gates = _mm([h], [w_in0], col, 2 * d, [b_branch_gate[0].reshape(1, 2 * d)],
                [pl.BlockSpec((1, tn), lambda j, i: (0, j))], "gate", tm, tn, BF16, "proj_gates")

    gain_a = (subln_a[0] * (1.0 - LAMBDA_INIT)).reshape(1, 2 * HEAD_DIM)
    o_a = _attn_a(qk_a, v_a, diff_lambda[0], gain_a, cfg)
    o_b = _attn_b(qk_b, v_b, cfg)

    tmm, tnm = cfg.tm_merge, cfg.tn_merge
    merged = _mm([o_a, o_b], [w_proj_a[0], w_proj_b[0]], 0, d, [gates, gates],
                 [pl.BlockSpec((tmm, tnm), lambda j, i: (i, j)),
                  pl.BlockSpec((tmm, tnm), lambda j, i: (i, d // tnm + j))],
                 "merge", tmm, tnm, BF16, "merge_branches")
    x1 = _mm([merged], [w_out[0]], 0, d, [x2], [pl.BlockSpec((tm, tn), lambda j, i: (i, j))],
             "residual", tm, tn, F32, "out_proj_residual")

    n_exp, tme = cfg.n_experts, cfg.tm_moe
    hp, sel, comb, rank, cnt = _router(x1, norm_ffn[0], w_router[0], b_router[0], cfg)
    n_tiles = (t * TOP_K) // tme + n_exp
    n_rows = n_tiles * tme
    counts = cnt[0].astype(jnp.int32)
    tiles_e = (counts + tme - 1) // tme
    tile_end = jnp.cumsum(tiles_e)
    row_start = (tile_end - tiles_e) * tme
    _, idx4 = lax.top_k(sel, TOP_K)
    dest_all = row_start[None, :] + rank.astype(jnp.int32)
    dest = jnp.take_along_axis(dest_all, idx4, axis=1)
    probs = jnp.take_along_axis(comb, idx4, axis=1)
    src = jnp.zeros((n_rows,), jnp.int32).at[dest.reshape(-1)].set(
        jnp.repeat(jnp.arange(t, dtype=jnp.int32), TOP_K))
    tile_ids = jnp.arange(n_tiles, dtype=jnp.int32)
    last_tile = tile_end[-1] - 1
    tile_expert = jnp.searchsorted(tile_end, jnp.minimum(tile_ids, last_tile), side="right").astype(jnp.int32)
    tile_valid = (tile_ids <= last_tile).astype(jnp.int32)

    xs = _dispatch(src, hp, n_rows, cfg)
    act = _expert_up(tile_expert, tile_valid, xs, w_gate[0], w_up[0], b_gate[0], b_up[0], cfg)
    ys = _expert_down(tile_expert, tile_valid, act, w_down[0], b_down[0], cfg)
    out = _combine(dest.reshape(-1), ys, probs, x1, norm_final, cfg)
    return out.reshape(b, s, d)


def kernel(x, w_in, b_branch_gate, norm_mix, subln_a, qnorm_b, knorm_b, diff_lambda, w_proj_a,
           w_proj_b, w_out, norm_ffn, w_router, b_router, w_gate, b_gate, w_up, b_up, w_down,
           b_down, norm_final):
    return _forward(FULL_CFG, x, w_in, b_branch_gate, norm_mix, subln_a, qnorm_b, knorm_b,
                    diff_lambda, w_proj_a, w_proj_b, w_out, norm_ffn, w_router, b_router, w_gate,
                    b_gate, w_up, b_up, w_down, b_down, norm_final)
```
